```python
import jax, jax.numpy as jnp
from jax import lax
import numpy as np

D_MODEL = 4096
BATCH = 2
SEQ = 8192
DEPTH = 1

CHUNK = 64
D_MIX = D_MODEL
M_HEADS = 4
M_V_DIM = D_MIX // 2 // M_HEADS
M_QK_DIM = M_V_DIM // 2
M_WIDTH = M_HEADS * M_V_DIM
G_HEADS = 16
G_HEAD_DIM = (D_MIX - M_WIDTH) // G_HEADS
G_WIDTH = G_HEADS * G_HEAD_DIM
CONV_K = 4
D_FF = 4 * D_MODEL
GATE_CAP = 15.0
NORM_EPS = 1e-6
L2_EPS = 1e-6
SPLIT_SIZES = (M_HEADS * M_QK_DIM, M_HEADS * M_QK_DIM, M_WIDTH, M_WIDTH, M_HEADS, M_HEADS,
               3 * G_WIDTH, G_WIDTH, G_HEADS, G_HEADS)
D_IN_PROJ = sum(SPLIT_SIZES)

kernel_name = 'hybrid_mlstm_gdn_parallel_heads'


def _rmsnorm(x, g):
    xf = x.astype(jnp.float32)
    y = xf * lax.rsqrt(jnp.mean(xf * xf, axis=-1, keepdims=True) + NORM_EPS)
    return (y * g.astype(jnp.float32)).astype(x.dtype)


def _soft_cap(z):
    return GATE_CAP * jnp.tanh(z / GATE_CAP)


def _l2norm(z):
    return z * lax.rsqrt(jnp.sum(z * z, axis=-1, keepdims=True) + L2_EPS)


def _to_chunks(t):
    b, t_len, h, d = t.shape
    return t.reshape(b, t_len // CHUNK, CHUNK, h, d).transpose(1, 0, 3, 2, 4)


def _gate_chunks(g):
    b, t_len, h = g.shape
    return g.reshape(b, t_len // CHUNK, CHUNK, h).transpose(1, 0, 3, 2)


def _from_chunks(y):
    nc, b, h, l, d = y.shape
    return y.transpose(1, 0, 3, 2, 4).reshape(b, nc * l, h, d)


def mlstm_chunkwise(q, k, v, i_pre, f_pre):
    b, _, h, dqk = q.shape
    dv = v.shape[-1]
    k = k * (dqk ** -0.5)
    log_i = _soft_cap(i_pre)
    log_f = jax.nn.log_sigmoid(_soft_cap(f_pre))
    causal = jnp.tril(jnp.ones((CHUNK, CHUNK), dtype=bool))

    def step(carry, inp):
        c_st, n_st, m_st = carry
        qc, kc, vc, ic, fc = inp
        bcum = jnp.cumsum(fc, axis=-1)
        d_mat = jnp.where(causal, bcum[..., :, None] - bcum[..., None, :] + ic[..., None, :], -jnp.inf)
        g_inter = bcum + m_st[..., None]
        m_t = jnp.maximum(g_inter, jnp.max(d_mat, axis=-1))
        w_inter = jnp.exp(g_inter - m_t)
        s_mat = jnp.einsum('bhld,bhsd->bhls', qc, kc) * jnp.exp(d_mat - m_t[..., None])
        num = (w_inter[..., None] * jnp.einsum('bhld,bhde->bhle', qc, c_st)
               + jnp.einsum('bhls,bhse->bhle', s_mat, vc))
        den = w_inter * jnp.einsum('bhld,bhd->bhl', qc, n_st) + jnp.sum(s_mat, axis=-1)
        h_out = num / jnp.maximum(jnp.abs(den), jnp.exp(-m_t))[..., None]
        b_last = bcum[..., -1]
        g_state = b_last[..., None] - bcum + ic
        m_new = jnp.maximum(b_last + m_st, jnp.max(g_state, axis=-1))
        w_old = jnp.exp(b_last + m_st - m_new)
        k_w = kc * jnp.exp(g_state - m_new[..., None])[..., None]
        c_new = w_old[..., None, None] * c_st + jnp.einsum('bhld,bhle->bhde', k_w, vc)
        n_new = w_old[..., None] * n_st + jnp.sum(k_w, axis=2)
        return (c_new, n_new, m_new), h_out

    init = (jnp.zeros((b, h, dqk, dv), jnp.float32),
            jnp.zeros((b, h, dqk), jnp.float32),
            jnp.zeros((b, h), jnp.float32))
    _, h_all = lax.scan(step, init, (_to_chunks(q), _to_chunks(k), _to_chunks(v),
                                     _gate_chunks(log_i), _gate_chunks(log_f)))
    return _from_chunks(h_all)


def causal_conv_silu(x, w):
    c = x.shape[-1]
    y = lax.conv_general_dilated(x, w[:, None, :], window_strides=(1,), padding=[(CONV_K - 1, 0)],
                                 dimension_numbers=('NWC', 'WIO', 'NWC'), feature_group_count=c)
    return jax.nn.silu(y)


def gated_deltanet_chunkwise(q, k, v, a_pre, b_pre, a_log, dt_bias):
    b, _, h, dk = q.shape
    dv = v.shape[-1]
    q = _l2norm(q) * (dk ** -0.5)
    k = _l2norm(k)
    beta = jax.nn.sigmoid(b_pre)
    g = -jnp.exp(a_log.astype(jnp.float32)) * jax.nn.softplus(a_pre + dt_bias.astype(jnp.float32))
    causal = jnp.tril(jnp.ones((CHUNK, CHUNK), dtype=bool))
    strict = jnp.tril(jnp.ones((CHUNK, CHUNK), dtype=bool), k=-1)
    eye = jnp.eye(CHUNK, dtype=jnp.float32)

    def step(s_st, inp):
        qc, kc, vc, gc, bc = inp
        decay = jnp.cumsum(gc, axis=-1)
        gam = jnp.exp(jnp.where(causal, decay[..., :, None] - decay[..., None, :], -jnp.inf))
        kb = kc * bc[..., None]
        vb = vc * bc[..., None]
        a_mat = jnp.where(strict, jnp.einsum('bhld,bhsd->bhls', kb, kc) * gam, 0.0)
        rhs = jnp.concatenate([vb, kb * jnp.exp(decay)[..., None]], axis=-1)
        sol = lax.linalg.triangular_solve(eye + a_mat, rhs, left_side=True, lower=True,
                                          unit_diagonal=True)
        u, w = sol[..., :dv], sol[..., dv:]
        v_new = u - jnp.einsum('bhld,bhde->bhle', w, s_st)
        attn = jnp.einsum('bhld,bhsd->bhls', qc, kc) * gam
        o = (jnp.einsum('bhld,bhde->bhle', qc * jnp.exp(decay)[..., None], s_st)
             + jnp.einsum('bhls,bhse->bhle', attn, v_new))
        d_last = decay[..., -1]
        k_dec = kc * jnp.exp(d_last[..., None] - decay)[..., None]
        s_new = s_st * jnp.exp(d_last)[..., None, None] + jnp.einsum('bhld,bhle->bhde', k_dec, v_new)
        return s_new, o

    init = jnp.zeros((b, h, dk, dv), jnp.float32)
    _, o_all = lax.scan(step, init, (_to_chunks(q), _to_chunks(k), _to_chunks(v),
                                     _gate_chunks(g), _gate_chunks(beta)))
    return _from_chunks(o_all)


def setup_inputs(seed: int = 0) -> dict:
    key = jax.random.key(seed)
    ks = jax.random.split(key, 16)
    nrm = jax.random.normal
    x = nrm(ks[0], (BATCH, SEQ, D_MODEL), jnp.float32)
    norm1_g = 1.0 + 0.02 * nrm(ks[1], (DEPTH, D_MODEL), jnp.float32)
    w_in = nrm(ks[2], (DEPTH, D_MODEL, D_IN_PROJ), jnp.float32) * (D_MODEL ** -0.5)
    mlstm_i_bias = 0.1 * nrm(ks[3], (DEPTH, M_HEADS), jnp.float32)
    mlstm_f_bias = 3.0 + 0.5 * nrm(ks[4], (DEPTH, M_HEADS), jnp.float32)
    mlstm_norm_g = 1.0 + 0.02 * nrm(ks[5], (DEPTH, M_WIDTH), jnp.float32)
    gdn_conv_w = nrm(ks[6], (DEPTH, CONV_K, 3 * G_WIDTH), jnp.float32) * (CONV_K ** -0.5)
    gdn_a_log = jnp.log(jax.random.uniform(ks[7], (DEPTH, G_HEADS), jnp.float32, 1.0, 16.0))
    dt = jnp.exp(jax.random.uniform(ks[8], (DEPTH, G_HEADS), jnp.float32,
                                    float(np.log(1e-3)), float(np.log(1e-1))))
    gdn_dt_bias = dt + jnp.log(-jnp.expm1(-dt))
    gdn_norm_g = 1.0 + 0.02 * nrm(ks[9], (DEPTH, G_HEAD_DIM), jnp.float32)
    w_out = nrm(ks[10], (DEPTH, D_MIX, D_MODEL), jnp.float32) * (D_MIX ** -0.5)
    norm2_g = 1.0 + 0.02 * nrm(ks[11], (DEPTH, D_MODEL), jnp.float32)
    w_up = nrm(ks[12], (DEPTH, D_MODEL, D_FF), jnp.float32) * (D_MODEL ** -0.5)
    w_down = nrm(ks[13], (DEPTH, D_FF, D_MODEL), jnp.float32) * (D_FF ** -0.5)
    norm_f_g = 1.0 + 0.02 * nrm(ks[14], (D_MODEL,), jnp.float32)
    return {'x': x, 'norm1_g': norm1_g, 'w_in': w_in, 'mlstm_i_bias': mlstm_i_bias,
            'mlstm_f_bias': mlstm_f_bias, 'mlstm_norm_g': mlstm_norm_g, 'gdn_conv_w': gdn_conv_w,
            'gdn_a_log': gdn_a_log, 'gdn_dt_bias': gdn_dt_bias, 'gdn_norm_g': gdn_norm_g,
            'w_out': w_out, 'norm2_g': norm2_g, 'w_up': w_up, 'w_down': w_down,
            'norm_f_g': norm_f_g}


def reference(x, norm1_g, w_in, mlstm_i_bias, mlstm_f_bias, mlstm_norm_g, gdn_conv_w,
              gdn_a_log, gdn_dt_bias, gdn_norm_g, w_out, norm2_g, w_up, w_down, norm_f_g):
    b, t_len, _ = x.shape
    f32 = jnp.float32
    split_points = np.cumsum(np.array(SPLIT_SIZES))[:-1].tolist()
    h = x
    for l in range(DEPTH):
        xn = _rmsnorm(h, norm1_g[l])
        proj = jnp.einsum('btd,de->bte', xn, w_in[l])
        m_q, m_k, m_v, m_o, m_i, m_f, g_qkv, g_z, g_a, g_b = jnp.split(proj, split_points, axis=-1)
        m_h = mlstm_chunkwise(
            m_q.astype(f32).reshape(b, t_len, M_HEADS, M_QK_DIM),
            m_k.astype(f32).reshape(b, t_len, M_HEADS, M_QK_DIM),
            m_v.astype(f32).reshape(b, t_len, M_HEADS, M_V_DIM),
            m_i.astype(f32) + mlstm_i_bias[l].astype(f32),
            m_f.astype(f32) + mlstm_f_bias[l].astype(f32))
        m_h = (_rmsnorm(m_h, mlstm_norm_g[l].reshape(M_HEADS, M_V_DIM))
               * jax.nn.sigmoid(m_o.astype(f32)).reshape(b, t_len, M_HEADS, M_V_DIM))
        g_qkv = causal_conv_silu(g_qkv.astype(f32), gdn_conv_w[l].astype(f32))
        g_q, g_k, g_v = jnp.split(g_qkv, 3, axis=-1)
        g_o = gated_deltanet_chunkwise(
            g_q.reshape(b, t_len, G_HEADS, G_HEAD_DIM),
            g_k.reshape(b, t_len, G_HEADS, G_HEAD_DIM),
            g_v.reshape(b, t_len, G_HEADS, G_HEAD_DIM),
            g_a.astype(f32), g_b.astype(f32), gdn_a_log[l], gdn_dt_bias[l])
        g_o = (_rmsnorm(g_o, gdn_norm_g[l])
               * jax.nn.silu(g_z.astype(f32)).reshape(b, t_len, G_HEADS, G_HEAD_DIM))
        mix = jnp.concatenate([m_h.reshape(b, t_len, M_WIDTH), g_o.reshape(b, t_len, G_WIDTH)],
                              axis=-1).astype(h.dtype)
        h = h + jnp.einsum('btm,md->btd', mix, w_out[l])
        hn = _rmsnorm(h, norm2_g[l])
        up = jnp.square(jax.nn.relu(jnp.einsum('btd,df->btf', hn, w_up[l])))
        h = h + jnp.einsum('btf,fd->btd', up, w_down[l])
    return _rmsnorm(h, norm_f_g)
```

```python
import functools

import jax
import jax.numpy as jnp
from jax import lax
from jax.experimental import pallas as pl
from jax.experimental.pallas import tpu as pltpu

F32 = jnp.float32
BF16 = jnp.bfloat16

D_MODEL = 4096
M_HEADS = 4
M_V_DIM = 512
M_QK_DIM = 256
M_WIDTH = M_HEADS * M_V_DIM
G_HEADS = 16
G_HEAD_DIM = 128
G_WIDTH = G_HEADS * G_HEAD_DIM
CONV_K = 4
D_FF = 4 * D_MODEL
GATE_CAP = 15.0
NORM_EPS = 1e-6
L2_EPS = 1e-6
GDN_CHUNK = 64

N_BIG = 2 * M_HEADS * M_QK_DIM + 2 * M_WIDTH + 4 * G_WIDTH
GATE_LANES = 128
COL_LOG_I = 0
COL_LOG_F = M_HEADS
COL_G = 2 * M_HEADS
COL_BETA = 2 * M_HEADS + G_HEADS
N_GATES = 2 * M_HEADS + 2 * G_HEADS

V7X_VMEM_LIMIT_BYTES = 56 * 1024 * 1024


def _cparams(semantics):
    return pltpu.CompilerParams(dimension_semantics=semantics,
                                vmem_limit_bytes=V7X_VMEM_LIMIT_BYTES)


def _softplus(z):
    return jnp.maximum(z, 0.0) + jnp.log1p(jnp.exp(-jnp.abs(z)))


def _rms_scale(z):
    return lax.rsqrt(jnp.mean(z * z, axis=-1, keepdims=True) + NORM_EPS)


def _norm_gates_kernel(x_ref, g_ref, wg_ref, bias_ref, alog_ref, xn_ref, gates_ref, gates_t_ref):
    x = x_ref[...]
    xn = (x * _rms_scale(x) * g_ref[...]).astype(BF16)
    xn_ref[...] = xn
    pre = jnp.dot(xn, wg_ref[...], preferred_element_type=F32) + bias_ref[...]
    lane = lax.broadcasted_iota(jnp.int32, pre.shape, 1)
    cap = GATE_CAP * jnp.tanh(pre / GATE_CAP)
    log_f = -_softplus(-cap)
    g = -jnp.exp(alog_ref[...]) * _softplus(pre)
    beta = jax.nn.sigmoid(pre)
    out = jnp.where(lane < COL_LOG_F, cap,
                    jnp.where(lane < COL_G, log_f,
                              jnp.where(lane < COL_BETA, g,
                                        jnp.where(lane < N_GATES, beta, 0.0))))
    gates_ref[...] = out
    gates_t_ref[...] = out.T


def _norm_gates(x2, g1, wg, bias_vec, alog_vec, *, tm=256):
    n = x2.shape[0]
    return pl.pallas_call(
        _norm_gates_kernel,
        grid=(n // tm,),
        in_specs=[
            pl.BlockSpec((tm, D_MODEL), lambda i: (i, 0)),
            pl.BlockSpec((1, D_MODEL), lambda i: (0, 0)),
            pl.BlockSpec((D_MODEL, GATE_LANES), lambda i: (0, 0)),
            pl.BlockSpec((1, GATE_LANES), lambda i: (0, 0)),
            pl.BlockSpec((1, GATE_LANES), lambda i: (0, 0)),
        ],
        out_specs=[
            pl.BlockSpec((tm, D_MODEL), lambda i: (i, 0)),
            pl.BlockSpec((tm, GATE_LANES), lambda i: (i, 0)),
            pl.BlockSpec((GATE_LANES, tm), lambda i: (0, i)),
        ],
        out_shape=[
            jax.ShapeDtypeStruct((n, D_MODEL), BF16),
            jax.ShapeDtypeStruct((n, GATE_LANES), F32),
            jax.ShapeDtypeStruct((GATE_LANES, n), F32),
        ],
        compiler_params=_cparams(("arbitrary",)),
        name="norm_gates",
    )(x2, g1, wg, bias_vec, alog_vec)


def _mm_kernel(a_ref, b_ref, o_ref):
    o_ref[...] = jnp.dot(a_ref[...], b_ref[...], preferred_element_type=F32).astype(o_ref.dtype)


def _matmul(a, b, *, bm, bn, out_dtype, name):
    m, k = a.shape
    n = b.shape[1]
    return pl.pallas_call(
        _mm_kernel,
        grid=(m // bm, n // bn),
        in_specs=[pl.BlockSpec((bm, k), lambda i, j: (i, 0)),
                  pl.BlockSpec((k, bn), lambda i, j: (0, j))],
        out_specs=pl.BlockSpec((bm, bn), lambda i, j: (i, j)),
        out_shape=jax.ShapeDtypeStruct((m, n), out_dtype),
        compiler_params=_cparams(("arbitrary", "arbitrary")),
        name=name,
    )(a, b)


def _mlstm_kernel(q_ref, k_ref, v_ref, o_ref, gates_ref, gates_t_ref, ng_ref, out_ref,
                  c_ref, n_ref, m_ref, *, chunk):
    h = pl.program_id(1)

    @pl.when(pl.program_id(2) == 0)
    def _():
        c_ref[...] = jnp.zeros_like(c_ref)
        n_ref[...] = jnp.zeros_like(n_ref)
        m_ref[...] = jnp.zeros_like(m_ref)

    q = q_ref[...]
    k = k_ref[...] * (M_QK_DIM ** -0.5)
    v = v_ref[...]
    gates = gates_ref[...]
    gates_t = gates_t_ref[...]
    lane = lax.broadcasted_iota(jnp.int32, gates.shape, 1)
    sub = lax.broadcasted_iota(jnp.int32, gates_t.shape, 0)
    row = lax.broadcasted_iota(jnp.int32, (chunk, chunk), 0)
    col = lax.broadcasted_iota(jnp.int32, (chunk, chunk), 1)
    causal = col <= row

    def pick_col(a, idx):
        return jnp.sum(jnp.where(lane == idx, a, 0.0), axis=1, keepdims=True)

    def pick_row(a, idx):
        return jnp.sum(jnp.where(sub == idx, a, 0.0), axis=0, keepdims=True)

    tril = causal.astype(F32)
    cum_cols = jnp.dot(tril, gates, preferred_element_type=F32, precision=lax.Precision.HIGHEST)
    bcum_c = pick_col(cum_cols, COL_LOG_F + h)
    ic_c = pick_col(gates, COL_LOG_I + h)
    triu = (row <= col).astype(F32)
    cum_rows = jnp.dot(gates_t, triu, preferred_element_type=F32, precision=lax.Precision.HIGHEST)
    bcum_r = pick_row(cum_rows, COL_LOG_F + h)
    ic_r = pick_row(gates_t, COL_LOG_I + h)

    m_st = m_ref[...]
    d_mat = jnp.where(causal, bcum_c - bcum_r + ic_r, -jnp.inf)
    g_inter = bcum_c + m_st
    m_t = jnp.maximum(g_inter, jnp.max(d_mat, axis=1, keepdims=True))
    w_inter = jnp.exp(g_inter - m_t)
    q_b = q.astype(BF16)
    k_b = k.astype(BF16)
    v_b = v.astype(BF16)
    s_mat = lax.dot_general(q_b, k_b, (((1,), (1,)), ((), ())),
                            preferred_element_type=F32) * jnp.exp(d_mat - m_t)
    qc = jnp.dot(q_b, c_ref[...].astype(BF16), preferred_element_type=F32)
    num = w_inter * qc + jnp.dot(s_mat.astype(BF16), v_b, preferred_element_type=F32)
    den = (w_inter * jnp.sum(q * n_ref[...], axis=1, keepdims=True)
           + jnp.sum(s_mat, axis=1, keepdims=True))
    h_out = num * (1.0 / jnp.maximum(jnp.abs(den), jnp.exp(-m_t)))

    b_last = bcum_c[chunk - 1:chunk, :]
    g_state = b_last - bcum_c + ic_c
    m_new = jnp.maximum(b_last + m_st, jnp.max(g_state, axis=0, keepdims=True))
    w_old = jnp.exp(b_last + m_st - m_new)
    k_w = k * jnp.exp(g_state - m_new)
    c_ref[...] = w_old * c_ref[...] + jnp.dot(k_w.T.astype(BF16), v_b, preferred_element_type=F32)
    n_ref[...] = w_old * n_ref[...] + jnp.sum(k_w, axis=0, keepdims=True)
    m_ref[...] = m_new

    y = h_out * _rms_scale(h_out) * ng_ref[...]
    out_ref[...] = (y * jax.nn.sigmoid(o_ref[...])).astype(out_ref.dtype)


def _mlstm(proj, gates, gates_t, norm_g, *, batch, seq, chunk=256):
    nc = seq // chunk
    n = batch * seq
    qk_blocks = M_HEADS
    v_block0 = 2 * M_HEADS * M_QK_DIM // M_V_DIM
    o_block0 = v_block0 + M_HEADS
    rowmap = lambda b, h, c: b * nc + c
    return pl.pallas_call(
        functools.partial(_mlstm_kernel, chunk=chunk),
        grid=(batch, M_HEADS, nc),
        in_specs=[
            pl.BlockSpec((chunk, M_QK_DIM), lambda b, h, c: (rowmap(b, h, c), h)),
            pl.BlockSpec((chunk, M_QK_DIM), lambda b, h, c: (rowmap(b, h, c), qk_blocks + h)),
            pl.BlockSpec((chunk, M_V_DIM), lambda b, h, c: (rowmap(b, h, c), v_block0 + h)),
            pl.BlockSpec((chunk, M_V_DIM), lambda b, h, c: (rowmap(b, h, c), o_block0 + h)),
            pl.BlockSpec((chunk, GATE_LANES), lambda b, h, c: (rowmap(b, h, c), 0)),
            pl.BlockSpec((8, chunk), lambda b, h, c: (0, rowmap(b, h, c))),
            pl.BlockSpec((None, 1, M_V_DIM), lambda b, h, c: (h, 0, 0)),
        ],
        out_specs=pl.BlockSpec((chunk, M_V_DIM), lambda b, h, c: (rowmap(b, h, c), h)),
        out_shape=jax.ShapeDtypeStruct((n, M_WIDTH), BF16),
        scratch_shapes=[pltpu.VMEM((M_QK_DIM, M_V_DIM), F32),
                        pltpu.VMEM((1, M_QK_DIM), F32),
                        pltpu.VMEM((1, 1), F32)],
        compiler_params=_cparams(("arbitrary", "arbitrary", "arbitrary")),
        name="mlstm",
    )(proj, proj, proj, proj, gates, gates_t, norm_g.reshape(M_HEADS, 1, M_V_DIM))


def _gdn_kernel(q_ref, k_ref, v_ref, z_ref, gates_ref, gates_t_ref, cw_ref, ng_ref, out_ref,
                xq_ref, xk_ref, xv_ref, s_ref, *, tb, hg):
    grp = pl.program_id(1)
    nsub = tb // GDN_CHUNK

    @pl.when(pl.program_id(2) == 0)
    def _():
        for r in (xq_ref, xk_ref, xv_ref):
            r[0:8, :] = jnp.zeros((8, r.shape[1]), F32)
        s_ref[...] = jnp.zeros_like(s_ref)

    def conv_silu(src_ref, buf_ref, w_index):
        buf_ref[8:8 + tb, :] = src_ref[...]
        acc = None
        for j in range(CONV_K):
            w = cw_ref[j:j + 1, w_index * hg * G_HEAD_DIM:(w_index + 1) * hg * G_HEAD_DIM]
            term = buf_ref[pl.ds(8 - (CONV_K - 1) + j, tb), :] * w
            acc = term if acc is None else acc + term
        buf_ref[0:8, :] = buf_ref[tb:tb + 8, :]
        return acc * jax.nn.sigmoid(acc)

    xq = conv_silu(q_ref, xq_ref, 0)
    xk = conv_silu(k_ref, xk_ref, 1)
    xv = conv_silu(v_ref, xv_ref, 2)

    gates = gates_ref[...]
    gates_t = gates_t_ref[...]
    row = lax.broadcasted_iota(jnp.int32, (tb, tb), 0)
    col = lax.broadcasted_iota(jnp.int32, (tb, tb), 1)
    same = jnp.right_shift(row, 6) == jnp.right_shift(col, 6)
    causal = same & (col <= row)
    strict = same & (col < row)
    eye = (row == col).astype(F32)
    cum_cols = jnp.dot(causal.astype(F32), gates, preferred_element_type=F32,
                       precision=lax.Precision.HIGHEST)
    cum_rows = jnp.dot(gates_t, (same & (row <= col)).astype(F32), preferred_element_type=F32,
                       precision=lax.Precision.HIGHEST)
    lane = lax.broadcasted_iota(jnp.int32, gates.shape, 1)
    sub = lax.broadcasted_iota(jnp.int32, gates_t.shape, 0)

    def pick_col(a, idx):
        return jnp.sum(jnp.where(lane == idx, a, 0.0), axis=1, keepdims=True)

    def pick_row(a, idx):
        return jnp.sum(jnp.where(sub == idx, a, 0.0), axis=0, keepdims=True)

    def dot_t(a, b):
        return lax.dot_general(a, b, (((1,), (1,)), ((), ())), preferred_element_type=F32)

    for hh in range(hg):
        head = grp * hg + hh
        sl = slice(hh * G_HEAD_DIM, (hh + 1) * G_HEAD_DIM)
        qh, kh, v = xq[:, sl], xk[:, sl], xv[:, sl]
        q = qh * lax.rsqrt(jnp.sum(qh * qh, axis=-1, keepdims=True) + L2_EPS) * (G_HEAD_DIM ** -0.5)
        k = kh * lax.rsqrt(jnp.sum(kh * kh, axis=-1, keepdims=True) + L2_EPS)
        beta = pick_col(gates, COL_BETA + head)
        dec_c = pick_col(cum_cols, COL_G + head)
        dec_r = pick_row(cum_rows, COL_G + head)
        gam = jnp.exp(jnp.where(causal, dec_c - dec_r, -jnp.inf))
        e_dec = jnp.exp(dec_c)
        kb = k * beta
        vb = v * beta
        k_b = k.astype(BF16)
        a_mat = jnp.where(strict, dot_t(kb.astype(BF16), k_b) * gam, 0.0)
        x_inv = eye - a_mat
        p = a_mat
        for _ in range(5):
            p_b = p.astype(BF16)
            p = jnp.dot(p_b, p_b, preferred_element_type=F32)
            x_inv = x_inv + jnp.dot(x_inv.astype(BF16), p.astype(BF16), preferred_element_type=F32)
        rhs = jnp.concatenate([vb, kb * e_dec], axis=-1).astype(BF16)
        sol = jnp.dot(x_inv.astype(BF16), rhs, preferred_element_type=F32)
        u, w = sol[:, :G_HEAD_DIM], sol[:, G_HEAD_DIM:]
        attn = dot_t(q.astype(BF16), k_b) * gam
        qd = (q * e_dec).astype(BF16)
        w_b = w.astype(BF16)

        s = s_ref[hh]
        v_new, o_inter = [], []
        for c in range(nsub):
            rs = slice(c * GDN_CHUNK, (c + 1) * GDN_CHUNK)
            s_b = s.astype(BF16)
            vn = u[rs] - jnp.dot(w_b[rs], s_b, preferred_element_type=F32)
            o_inter.append(jnp.dot(qd[rs], s_b, preferred_element_type=F32))
            d_last = dec_c[(c + 1) * GDN_CHUNK - 1:(c + 1) * GDN_CHUNK, :]
            k_dec = k[rs] * jnp.exp(d_last - dec_c[rs])
            s = s * jnp.exp(d_last) + jnp.dot(k_dec.T.astype(BF16), vn.astype(BF16),
                                              preferred_element_type=F32)
            v_new.append(vn)
        s_ref[hh] = s
        o = (jnp.concatenate(o_inter, axis=0)
             + jnp.dot(attn.astype(BF16), jnp.concatenate(v_new, axis=0).astype(BF16),
                       preferred_element_type=F32))
        y = o * _rms_scale(o) * ng_ref[...]
        zh = z_ref[:, sl]
        out_ref[:, sl] = (y * (zh * jax.nn.sigmoid(zh))).astype(out_ref.dtype)


def _gdn(proj, gates, gates_t, conv_w, norm_g, *, batch, seq, tb=128, hg=4):
    nt = seq // tb
    n = batch * seq
    ngrp = G_HEADS // hg
    wcols = hg * G_HEAD_DIM
    qkv0 = (2 * M_HEADS * M_QK_DIM + 2 * M_WIDTH) // wcols
    rowmap = lambda b, g, t: b * nt + t
    return pl.pallas_call(
        functools.partial(_gdn_kernel, tb=tb, hg=hg),
        grid=(batch, ngrp, nt),
        in_specs=[
            pl.BlockSpec((tb, wcols), lambda b, g, t: (rowmap(b, g, t), qkv0 + g)),
            pl.BlockSpec((tb, wcols), lambda b, g, t: (rowmap(b, g, t), qkv0 + ngrp + g)),
            pl.BlockSpec((tb, wcols), lambda b, g, t: (rowmap(b, g, t), qkv0 + 2 * ngrp + g)),
            pl.BlockSpec((tb, wcols), lambda b, g, t: (rowmap(b, g, t), qkv0 + 3 * ngrp + g)),
            pl.BlockSpec((tb, GATE_LANES), lambda b, g, t: (rowmap(b, g, t), 0)),
            pl.BlockSpec((GATE_LANES, tb), lambda b, g, t: (0, rowmap(b, g, t))),
            pl.BlockSpec((None, 8, 3 * wcols), lambda b, g, t: (g, 0, 0)),
            pl.BlockSpec((1, G_HEAD_DIM), lambda b, g, t: (0, 0)),
        ],
        out_specs=pl.BlockSpec((tb, wcols), lambda b, g, t: (rowmap(b, g, t), g)),
        out_shape=jax.ShapeDtypeStruct((n, G_WIDTH), BF16),
        scratch_shapes=[pltpu.VMEM((tb + 8, wcols), F32),
                        pltpu.VMEM((tb + 8, wcols), F32),
                        pltpu.VMEM((tb + 8, wcols), F32),
                        pltpu.VMEM((hg, G_HEAD_DIM, G_HEAD_DIM), F32)],
        compiler_params=_cparams(("arbitrary", "arbitrary", "arbitrary")),
        name="gdn",
    )(proj, proj, proj, proj, gates, gates_t, conv_w, norm_g)


def _out_proj_kernel(am_ref, ag_ref, b_ref, x_ref, o_ref):
    acc = jnp.dot(am_ref[...], b_ref[0:M_WIDTH, :], preferred_element_type=F32)
    acc = acc + jnp.dot(ag_ref[...], b_ref[M_WIDTH:, :], preferred_element_type=F32)
    o_ref[...] = x_ref[...] + acc


def _out_proj(mix_m, mix_g, w_out, x2, *, bm=1024, bn=1024):
    n = x2.shape[0]
    return pl.pallas_call(
        _out_proj_kernel,
        grid=(n // bm, D_MODEL // bn),
        in_specs=[pl.BlockSpec((bm, M_WIDTH), lambda i, j: (i, 0)),
                  pl.BlockSpec((bm, G_WIDTH), lambda i, j: (i, 0)),
                  pl.BlockSpec((M_WIDTH + G_WIDTH, bn), lambda i, j: (0, j)),
                  pl.BlockSpec((bm, bn), lambda i, j: (i, j))],
        out_specs=pl.BlockSpec((bm, bn), lambda i, j: (i, j)),
        out_shape=jax.ShapeDtypeStruct((n, D_MODEL), F32),
        compiler_params=_cparams(("arbitrary", "arbitrary")),
        name="out_proj",
    )(mix_m, mix_g, w_out, x2)


def _rmsnorm_kernel(x_ref, g_ref, o_ref):
    x = x_ref[...]
    o_ref[...] = (x * _rms_scale(x) * g_ref[...]).astype(o_ref.dtype)


def _rmsnorm(x2, g, out_dtype, *, tm=256, name):
    n = x2.shape[0]
    return pl.pallas_call(
        _rmsnorm_kernel,
        grid=(n // tm,),
        in_specs=[pl.BlockSpec((tm, D_MODEL), lambda i: (i, 0)),
                  pl.BlockSpec((1, D_MODEL), lambda i: (0, 0))],
        out_specs=pl.BlockSpec((tm, D_MODEL), lambda i: (i, 0)),
        out_shape=jax.ShapeDtypeStruct((n, D_MODEL), out_dtype),
        compiler_params=_cparams(("arbitrary",)),
        name=name,
    )(x2, g)


def _up_kernel(a_ref, b_ref, o_ref):
    u = jnp.maximum(jnp.dot(a_ref[...], b_ref[...], preferred_element_type=F32), 0.0)
    o_ref[...] = (u * u).astype(o_ref.dtype)


def _up_proj(hn, w_up, *, bm=1024, bn=1024):
    n = hn.shape[0]
    return pl.pallas_call(
        _up_kernel,
        grid=(n // bm, D_FF // bn),
        in_specs=[pl.BlockSpec((bm, D_MODEL), lambda i, j: (i, 0)),
                  pl.BlockSpec((D_MODEL, bn), lambda i, j: (0, j))],
        out_specs=pl.BlockSpec((bm, bn), lambda i, j: (i, j)),
        out_shape=jax.ShapeDtypeStruct((n, D_FF), BF16),
        compiler_params=_cparams(("arbitrary", "arbitrary")),
        name="up_proj",
    )(hn, w_up)


def _down_kernel(a_ref, b_ref, r_ref, o_ref, acc_ref):
    kk = pl.program_id(2)

    @pl.when(kk == 0)
    def _():
        acc_ref[...] = r_ref[...]

    acc_ref[...] += jnp.dot(a_ref[...], b_ref[...], preferred_element_type=F32)

    @pl.when(kk == pl.num_programs(2) - 1)
    def _():
        o_ref[...] = acc_ref[...]


def _down_proj(u, w_down, resid, *, bm=1024, bn=1024, bk=2048):
    n = u.shape[0]
    return pl.pallas_call(
        _down_kernel,
        grid=(n // bm, D_MODEL // bn, D_FF // bk),
        in_specs=[pl.BlockSpec((bm, bk), lambda i, j, k: (i, k)),
                  pl.BlockSpec((bk, bn), lambda i, j, k: (k, j)),
                  pl.BlockSpec((bm, bn), lambda i, j, k: (i, j))],
        out_specs=pl.BlockSpec((bm, bn), lambda i, j, k: (i, j)),
        out_shape=jax.ShapeDtypeStruct((n, D_MODEL), F32),
        scratch_shapes=[pltpu.VMEM((bm, bn), F32)],
        compiler_params=_cparams(("arbitrary", "arbitrary", "arbitrary")),
        name="down_proj",
    )(u, w_down, resid)


def _pad_lanes(vec):
    return jnp.pad(vec, (0, GATE_LANES - vec.shape[0])).reshape(1, GATE_LANES)


def kernel(x, norm1_g, w_in, mlstm_i_bias, mlstm_f_bias, mlstm_norm_g, gdn_conv_w, gdn_a_log,
           gdn_dt_bias, gdn_norm_g, w_out, norm2_g, w_up, w_down, norm_f_g):
    batch, seq, _ = x.shape
    depth = w_in.shape[0]
    n = batch * seq
    h = x.reshape(n, D_MODEL)
    gate0 = 2 * M_HEADS * M_QK_DIM + 2 * M_WIDTH
    gqkv0 = gate0 + 2 * M_HEADS
    ga0 = gqkv0 + 4 * G_WIDTH
    hg = 4
    for l in range(depth):
        wl = w_in[l]
        w_big = jnp.concatenate([wl[:, :gate0], wl[:, gqkv0:ga0]], axis=1).astype(BF16)
        w_gates = jnp.pad(jnp.concatenate([wl[:, gate0:gqkv0], wl[:, ga0:]], axis=1),
                          ((0, 0), (0, GATE_LANES - N_GATES))).astype(BF16)
        bias_vec = _pad_lanes(jnp.concatenate([mlstm_i_bias[l], mlstm_f_bias[l], gdn_dt_bias[l]]))
        alog_vec = _pad_lanes(jnp.concatenate([jnp.zeros((2 * M_HEADS,), F32), gdn_a_log[l]]))
        cw = gdn_conv_w[l].reshape(CONV_K, 3, G_HEADS // hg, hg * G_HEAD_DIM)
        cw = jnp.pad(cw.transpose(2, 0, 1, 3).reshape(G_HEADS // hg, CONV_K, 3 * hg * G_HEAD_DIM),
                     ((0, 0), (0, 8 - CONV_K), (0, 0)))

        xn, gates, gates_t = _norm_gates(h, norm1_g[l].reshape(1, D_MODEL), w_gates, bias_vec, alog_vec)
        proj = _matmul(xn, w_big, bm=1024, bn=1024, out_dtype=F32, name="in_proj")
        mix_m = _mlstm(proj, gates, gates_t, mlstm_norm_g[l], batch=batch, seq=seq)
        mix_g = _gdn(proj, gates, gates_t, cw, gdn_norm_g[l].reshape(1, G_HEAD_DIM),
                     batch=batch, seq=seq, hg=hg)
        h = _out_proj(mix_m, mix_g, w_out[l].astype(BF16), h)
        hn = _rmsnorm(h, norm2_g[l].reshape(1, D_MODEL), BF16, name="norm2")
        up = _up_proj(hn, w_up[l].astype(BF16))
        h = _down_proj(up, w_down[l].astype(BF16), h)
    out = _rmsnorm(h, norm_f_g.reshape(1, D_MODEL), F32, name="norm_f")
    return out.reshape(batch, seq, D_MODEL)
```

```python
import functools

import jax
import jax.numpy as jnp
from jax import lax
from jax.experimental import pallas as pl
from jax.experimental.pallas import tpu as pltpu

F32 = jnp.float32
BF16 = jnp.bfloat16

D_MODEL = 4096
M_HEADS = 4
M_V_DIM = 512
M_QK_DIM = 256
M_WIDTH = M_HEADS * M_V_DIM
G_HEADS = 16
G_HEAD_DIM = 128
G_WIDTH = G_HEADS * G_HEAD_DIM
CONV_K = 4
D_FF = 4 * D_MODEL
GATE_CAP = 15.0
NORM_EPS = 1e-6
L2_EPS = 1e-6
GDN_CHUNK = 64

N_BIG = 2 * M_HEADS * M_QK_DIM + 2 * M_WIDTH + 4 * G_WIDTH
GATE_LANES = 128
COL_LOG_I = 0
COL_LOG_F = M_HEADS
COL_G = 2 * M_HEADS
COL_BETA = 2 * M_HEADS + G_HEADS
N_GATES = 2 * M_HEADS + 2 * G_HEADS

V7X_VMEM_LIMIT_BYTES = 56 * 1024 * 1024


def _cparams(semantics):
    return pltpu.CompilerParams(dimension_semantics=semantics,
                                vmem_limit_bytes=V7X_VMEM_LIMIT_BYTES)


def _softplus(z):
    return jnp.maximum(z, 0.0) + jnp.log1p(jnp.exp(-jnp.abs(z)))


def _rms_scale(z):
    return lax.rsqrt(jnp.mean(z * z, axis=-1, keepdims=True) + NORM_EPS)


def _norm_gates_kernel(x_ref, g_ref, wg_ref, bias_ref, alog_ref, xn_ref, gates_ref, gates_t_ref):
    x = x_ref[...]
    xn = (x * _rms_scale(x) * g_ref[...]).astype(BF16)
    xn_ref[...] = xn
    pre = jnp.dot(xn, wg_ref[...], preferred_element_type=F32) + bias_ref[...]
    lane = lax.broadcasted_iota(jnp.int32, pre.shape, 1)
    cap = GATE_CAP * jnp.tanh(pre / GATE_CAP)
    log_f = -_softplus(-cap)
    g = -jnp.exp(alog_ref[...]) * _softplus(pre)
    beta = jax.nn.sigmoid(pre)
    out = jnp.where(lane < COL_LOG_F, cap,
                    jnp.where(lane < COL_G, log_f,
                              jnp.where(lane < COL_BETA, g,
                                        jnp.where(lane < N_GATES, beta, 0.0))))
    gates_ref[...] = out
    gates_t_ref[...] = out.T


def _norm_gates(x2, g1, wg, bias_vec, alog_vec, *, tm=256):
    n = x2.shape[0]
    return pl.pallas_call(
        _norm_gates_kernel,
        grid=(n // tm,),
        in_specs=[
            pl.BlockSpec((tm, D_MODEL), lambda i: (i, 0)),
            pl.BlockSpec((1, D_MODEL), lambda i: (0, 0)),
            pl.BlockSpec((D_MODEL, GATE_LANES), lambda i: (0, 0)),
            pl.BlockSpec((1, GATE_LANES), lambda i: (0, 0)),
            pl.BlockSpec((1, GATE_LANES), lambda i: (0, 0)),
        ],
        out_specs=[
            pl.BlockSpec((tm, D_MODEL), lambda i: (i, 0)),
            pl.BlockSpec((tm, GATE_LANES), lambda i: (i, 0)),
            pl.BlockSpec((GATE_LANES, tm), lambda i: (0, i)),
        ],
        out_shape=[
            jax.ShapeDtypeStruct((n, D_MODEL), BF16),
            jax.ShapeDtypeStruct((n, GATE_LANES), F32),
            jax.ShapeDtypeStruct((GATE_LANES, n), F32),
        ],
        compiler_params=_cparams(("arbitrary",)),
        name="norm_gates",
    )(x2, g1, wg, bias_vec, alog_vec)


def _mm_kernel(a_ref, b_ref, o_ref):
    o_ref[...] = jnp.dot(a_ref[...], b_ref[...], preferred_element_type=F32).astype(o_ref.dtype)


def _matmul(a, b, *, bm, bn, out_dtype, name):
    m, k = a.shape
    n = b.shape[1]
    return pl.pallas_call(
        _mm_kernel,
        grid=(m // bm, n // bn),
        in_specs=[pl.BlockSpec((bm, k), lambda i, j: (i, 0)),
                  pl.BlockSpec((k, bn), lambda i, j: (0, j))],
        out_specs=pl.BlockSpec((bm, bn), lambda i, j: (i, j)),
        out_shape=jax.ShapeDtypeStruct((m, n), out_dtype),
        compiler_params=_cparams(("arbitrary", "arbitrary")),
        name=name,
    )(a, b)


def _mlstm_kernel(q_ref, k_ref, v_ref, o_ref, gates_ref, gates_t_ref, ng_ref, out_ref,
                  c_ref, n_ref, m_ref, *, chunk):
    h = pl.program_id(1)

    @pl.when(pl.program_id(2) == 0)
    def _():
        c_ref[...] = jnp.zeros_like(c_ref)
        n_ref[...] = jnp.zeros_like(n_ref)
        m_ref[...] = jnp.zeros_like(m_ref)

    q = q_ref[...]
    k = k_ref[...] * (M_QK_DIM ** -0.5)
    v = v_ref[...]
    gates = gates_ref[...]
    gates_t = gates_t_ref[...]
    lane = lax.broadcasted_iota(jnp.int32, gates.shape, 1)
    sub = lax.broadcasted_iota(jnp.int32, gates_t.shape, 0)
    row = lax.broadcasted_iota(jnp.int32, (chunk, chunk), 0)
    col = lax.broadcasted_iota(jnp.int32, (chunk, chunk), 1)
    causal = col <= row

    def pick_col(a, idx):
        return jnp.sum(jnp.where(lane == idx, a, 0.0), axis=1, keepdims=True)

    def pick_row(a, idx):
        return jnp.sum(jnp.where(sub == idx, a, 0.0), axis=0, keepdims=True)

    tril = causal.astype(F32)
    cum_cols = jnp.dot(tril, gates, preferred_element_type=F32, precision=lax.Precision.HIGHEST)
    bcum_c = pick_col(cum_cols, COL_LOG_F + h)
    ic_c = pick_col(gates, COL_LOG_I + h)
    triu = (row <= col).astype(F32)
    cum_rows = jnp.dot(gates_t, triu, preferred_element_type=F32, precision=lax.Precision.HIGHEST)
    bcum_r = pick_row(cum_rows, COL_LOG_F + h)
    ic_r = pick_row(gates_t, COL_LOG_I + h)

    m_st = m_ref[...]
    d_mat = jnp.where(causal, bcum_c - bcum_r + ic_r, -jnp.inf)
    g_inter = bcum_c + m_st
    m_t = jnp.maximum(g_inter, jnp.max(d_mat, axis=1, keepdims=True))
    w_inter = jnp.exp(g_inter - m_t)
    q_b = q.astype(BF16)
    k_b = k.astype(BF16)
    v_b = v.astype(BF16)
    s_mat = lax.dot_general(q_b, k_b, (((1,), (1,)), ((), ())),
                            preferred_element_type=F32) * jnp.exp(d_mat - m_t)
    qc = jnp.dot(q_b, c_ref[...].astype(BF16), preferred_element_type=F32)
    num = w_inter * qc + jnp.dot(s_mat.astype(BF16), v_b, preferred_element_type=F32)
    den = (w_inter * jnp.sum(q * n_ref[...], axis=1, keepdims=True)
           + jnp.sum(s_mat, axis=1, keepdims=True))
    h_out = num * (1.0 / jnp.maximum(jnp.abs(den), jnp.exp(-m_t)))

    b_last = bcum_c[chunk - 1:chunk, :]
    g_state = b_last - bcum_c + ic_c
    m_new = jnp.maximum(b_last + m_st, jnp.max(g_state, axis=0, keepdims=True))
    w_old = jnp.exp(b_last + m_st - m_new)
    k_w = k * jnp.exp(g_state - m_new)
    c_ref[...] = w_old * c_ref[...] + jnp.dot(k_w.T.astype(BF16), v_b, preferred_element_type=F32)
    n_ref[...] = w_old * n_ref[...] + jnp.sum(k_w, axis=0, keepdims=True)
    m_ref[...] = m_new

    y = h_out * _rms_scale(h_out) * ng_ref[...]
    out_ref[...] = (y * jax.nn.sigmoid(o_ref[...])).astype(out_ref.dtype)


def _mlstm(proj, gates, gates_t, norm_g, *, batch, seq, chunk=256):
    nc = seq // chunk
    n = batch * seq
    qk_blocks = M_HEADS
    v_block0 = 2 * M_HEADS * M_QK_DIM // M_V_DIM
    o_block0 = v_block0 + M_HEADS
    rowmap = lambda b, h, c: b * nc + c
    return pl.pallas_call(
        functools.partial(_mlstm_kernel, chunk=chunk),
        grid=(batch, M_HEADS, nc),
        in_specs=[
            pl.BlockSpec((chunk, M_QK_DIM), lambda b, h, c: (rowmap(b, h, c), h)),
            pl.BlockSpec((chunk, M_QK_DIM), lambda b, h, c: (rowmap(b, h, c), qk_blocks + h)),
            pl.BlockSpec((chunk, M_V_DIM), lambda b, h, c: (rowmap(b, h, c), v_block0 + h)),
            pl.BlockSpec((chunk, M_V_DIM), lambda b, h, c: (rowmap(b, h, c), o_block0 + h)),
            pl.BlockSpec((chunk, GATE_LANES), lambda b, h, c: (rowmap(b, h, c), 0)),
            pl.BlockSpec((8, chunk), lambda b, h, c: (0, rowmap(b, h, c))),
            pl.BlockSpec((None, 1, M_V_DIM), lambda b, h, c: (h, 0, 0)),
        ],
        out_specs=pl.BlockSpec((chunk, M_V_DIM), lambda b, h, c: (rowmap(b, h, c), h)),
        out_shape=jax.ShapeDtypeStruct((n, M_WIDTH), BF16),
        scratch_shapes=[pltpu.VMEM((M_QK_DIM, M_V_DIM), F32),
                        pltpu.VMEM((1, M_QK_DIM), F32),
                        pltpu.VMEM((1, 1), F32)],
        compiler_params=_cparams(("arbitrary", "arbitrary", "arbitrary")),
        name="mlstm",
    )(proj, proj, proj, proj, gates, gates_t, norm_g.reshape(M_HEADS, 1, M_V_DIM))


def _gdn_kernel(q_ref, k_ref, v_ref, z_ref, gates_ref, gates_t_ref, cw_ref, ng_ref, out_ref,
                xq_ref, xk_ref, xv_ref, s_ref, *, tb, hg):
    grp = pl.program_id(1)
    nsub = tb // GDN_CHUNK

    @pl.when(pl.program_id(2) == 0)
    def _():
        for r in (xq_ref, xk_ref, xv_ref):
            r[0:8, :] = jnp.zeros((8, r.shape[1]), F32)
        s_ref[...] = jnp.zeros_like(s_ref)

    def conv_silu(src_ref, buf_ref, w_index):
        buf_ref[8:8 + tb, :] = src_ref[...]
        acc = None
        for j in range(CONV_K):
            w = cw_ref[j:j + 1, w_index * hg * G_HEAD_DIM:(w_index + 1) * hg * G_HEAD_DIM]
            term = buf_ref[pl.ds(8 - (CONV_K - 1) + j, tb), :] * w
            acc = term if acc is None else acc + term
        buf_ref[0:8, :] = buf_ref[tb:tb + 8, :]
        return acc * jax.nn.sigmoid(acc)

    xq = conv_silu(q_ref, xq_ref, 0)
    xk = conv_silu(k_ref, xk_ref, 1)
    xv = conv_silu(v_ref, xv_ref, 2)

    gates = gates_ref[...]
    gates_t = gates_t_ref[...]
    row = lax.broadcasted_iota(jnp.int32, (tb, tb), 0)
    col = lax.broadcasted_iota(jnp.int32, (tb, tb), 1)
    same = jnp.right_shift(row, 6) == jnp.right_shift(col, 6)
    causal = same & (col <= row)
    strict = same & (col < row)
    eye = (row == col).astype(F32)
    cum_cols = jnp.dot(causal.astype(F32), gates, preferred_element_type=F32,
                       precision=lax.Precision.HIGHEST)
    cum_rows = jnp.dot(gates_t, (same & (row <= col)).astype(F32), preferred_element_type=F32,
                       precision=lax.Precision.HIGHEST)
    lane = lax.broadcasted_iota(jnp.int32, gates.shape, 1)
    sub = lax.broadcasted_iota(jnp.int32, gates_t.shape, 0)

    def pick_col(a, idx):
        return jnp.sum(jnp.where(lane == idx, a, 0.0), axis=1, keepdims=True)

    def pick_row(a, idx):
        return jnp.sum(jnp.where(sub == idx, a, 0.0), axis=0, keepdims=True)

    def dot_t(a, b):
        return lax.dot_general(a, b, (((1,), (1,)), ((), ())), preferred_element_type=F32)

    def dot(a, b):
        return jnp.dot(a, b, preferred_element_type=F32)

    def l2n(z):
        return z * lax.rsqrt(jnp.sum(z * z, axis=-1, keepdims=True) + L2_EPS)

    hs = range(hg)
    sls = [slice(hh * G_HEAD_DIM, (hh + 1) * G_HEAD_DIM) for hh in hs]
    q = [l2n(xq[:, sl]) * (G_HEAD_DIM ** -0.5) for sl in sls]
    k = [l2n(xk[:, sl]) for sl in sls]
    beta = [pick_col(gates, COL_BETA + grp * hg + hh) for hh in hs]
    dec_c = [pick_col(cum_cols, COL_G + grp * hg + hh) for hh in hs]
    dec_r = [pick_row(cum_rows, COL_G + grp * hg + hh) for hh in hs]
    gam = [jnp.exp(jnp.where(causal, dec_c[h] - dec_r[h], -jnp.inf)) for h in hs]
    e_dec = [jnp.exp(dec_c[h]) for h in hs]
    kb = [k[h] * beta[h] for h in hs]
    k_b = [k[h].astype(BF16) for h in hs]
    a_mat = [jnp.where(strict, dot_t(kb[h].astype(BF16), k_b[h]) * gam[h], 0.0) for h in hs]
    x_inv = [eye - a_mat[h] for h in hs]
    p_b = [a_mat[h].astype(BF16) for h in hs]
    for _ in range(5):
        p = [dot(p_b[h], p_b[h]) for h in hs]
        p_b = [p[h].astype(BF16) for h in hs]
        x_inv = [x_inv[h] + dot(x_inv[h].astype(BF16), p_b[h]) for h in hs]
    rhs = [jnp.concatenate([xv[:, sls[h]] * beta[h], kb[h] * e_dec[h]], axis=-1).astype(BF16)
           for h in hs]
    sol = [dot(x_inv[h].astype(BF16), rhs[h]) for h in hs]
    u = [sol[h][:, :G_HEAD_DIM] for h in hs]
    w_b = [sol[h][:, G_HEAD_DIM:].astype(BF16) for h in hs]
    attn_b = [(dot_t(q[h].astype(BF16), k_b[h]) * gam[h]).astype(BF16) for h in hs]
    qd = [(q[h] * e_dec[h]).astype(BF16) for h in hs]

    s = [s_ref[h] for h in hs]
    v_new = [[] for _ in hs]
    o_inter = [[] for _ in hs]
    for c in range(nsub):
        rs = slice(c * GDN_CHUNK, (c + 1) * GDN_CHUNK)
        s_b = [s[h].astype(BF16) for h in hs]
        vn = [u[h][rs] - dot(w_b[h][rs], s_b[h]) for h in hs]
        for h in hs:
            o_inter[h].append(dot(qd[h][rs], s_b[h]))
            v_new[h].append(vn[h])
        d_last = [dec_c[h][(c + 1) * GDN_CHUNK - 1:(c + 1) * GDN_CHUNK, :] for h in hs]
        k_dec = [(k[h][rs] * jnp.exp(d_last[h] - dec_c[h][rs])).T.astype(BF16) for h in hs]
        s = [s[h] * jnp.exp(d_last[h]) + dot(k_dec[h], vn[h].astype(BF16)) for h in hs]
    for h in hs:
        s_ref[h] = s[h]
    o = [jnp.concatenate(o_inter[h], axis=0)
         + dot(attn_b[h], jnp.concatenate(v_new[h], axis=0).astype(BF16)) for h in hs]
    for h in hs:
        y = o[h] * _rms_scale(o[h]) * ng_ref[...]
        zh = z_ref[:, sls[h]]
        out_ref[:, sls[h]] = (y * (zh * jax.nn.sigmoid(zh))).astype(out_ref.dtype)


def _gdn(proj, gates, gates_t, conv_w, norm_g, *, batch, seq, tb=128, hg=4):
    nt = seq // tb
    n = batch * seq
    ngrp = G_HEADS // hg
    wcols = hg * G_HEAD_DIM
    qkv0 = (2 * M_HEADS * M_QK_DIM + 2 * M_WIDTH) // wcols
    rowmap = lambda b, g, t: b * nt + t
    return pl.pallas_call(
        functools.partial(_gdn_kernel, tb=tb, hg=hg),
        grid=(batch, ngrp, nt),
        in_specs=[
            pl.BlockSpec((tb, wcols), lambda b, g, t: (rowmap(b, g, t), qkv0 + g)),
            pl.BlockSpec((tb, wcols), lambda b, g, t: (rowmap(b, g, t), qkv0 + ngrp + g)),
            pl.BlockSpec((tb, wcols), lambda b, g, t: (rowmap(b, g, t), qkv0 + 2 * ngrp + g)),
            pl.BlockSpec((tb, wcols), lambda b, g, t: (rowmap(b, g, t), qkv0 + 3 * ngrp + g)),
            pl.BlockSpec((tb, GATE_LANES), lambda b, g, t: (rowmap(b, g, t), 0)),
            pl.BlockSpec((GATE_LANES, tb), lambda b, g, t: (0, rowmap(b, g, t))),
            pl.BlockSpec((None, 8, 3 * wcols), lambda b, g, t: (g, 0, 0)),
            pl.BlockSpec((1, G_HEAD_DIM), lambda b, g, t: (0, 0)),
        ],
        out_specs=pl.BlockSpec((tb, wcols), lambda b, g, t: (rowmap(b, g, t), g)),
        out_shape=jax.ShapeDtypeStruct((n, G_WIDTH), BF16),
        scratch_shapes=[pltpu.VMEM((tb + 8, wcols), F32),
                        pltpu.VMEM((tb + 8, wcols), F32),
                        pltpu.VMEM((tb + 8, wcols), F32),
                        pltpu.VMEM((hg, G_HEAD_DIM, G_HEAD_DIM), F32)],
        compiler_params=_cparams(("arbitrary", "arbitrary", "arbitrary")),
        name="gdn",
    )(proj, proj, proj, proj, gates, gates_t, conv_w, norm_g)


def _out_proj_kernel(am_ref, ag_ref, b_ref, x_ref, o_ref):
    acc = jnp.dot(am_ref[...], b_ref[0:M_WIDTH, :], preferred_element_type=F32)
    acc = acc + jnp.dot(ag_ref[...], b_ref[M_WIDTH:, :], preferred_element_type=F32)
    o_ref[...] = x_ref[...] + acc


def _out_proj(mix_m, mix_g, w_out, x2, *, bm=1024, bn=1024):
    n = x2.shape[0]
    return pl.pallas_call(
        _out_proj_kernel,
        grid=(n // bm, D_MODEL // bn),
        in_specs=[pl.BlockSpec((bm, M_WIDTH), lambda i, j: (i, 0)),
                  pl.BlockSpec((bm, G_WIDTH), lambda i, j: (i, 0)),
                  pl.BlockSpec((M_WIDTH + G_WIDTH, bn), lambda i, j: (0, j)),
                  pl.BlockSpec((bm, bn), lambda i, j: (i, j))],
        out_specs=pl.BlockSpec((bm, bn), lambda i, j: (i, j)),
        out_shape=jax.ShapeDtypeStruct((n, D_MODEL), F32),
        compiler_params=_cparams(("arbitrary", "arbitrary")),
        name="out_proj",
    )(mix_m, mix_g, w_out, x2)


def _rmsnorm_kernel(x_ref, g_ref, o_ref):
    x = x_ref[...]
    o_ref[...] = (x * _rms_scale(x) * g_ref[...]).astype(o_ref.dtype)


def _rmsnorm(x2, g, out_dtype, *, tm=256, name):
    n = x2.shape[0]
    return pl.pallas_call(
        _rmsnorm_kernel,
        grid=(n // tm,),
        in_specs=[pl.BlockSpec((tm, D_MODEL), lambda i: (i, 0)),
                  pl.BlockSpec((1, D_MODEL), lambda i: (0, 0))],
        out_specs=pl.BlockSpec((tm, D_MODEL), lambda i: (i, 0)),
        out_shape=jax.ShapeDtypeStruct((n, D_MODEL), out_dtype),
        compiler_params=_cparams(("arbitrary",)),
        name=name,
    )(x2, g)


def _up_kernel(a_ref, b_ref, o_ref):
    u = jnp.maximum(jnp.dot(a_ref[...], b_ref[...], preferred_element_type=F32), 0.0)
    o_ref[...] = (u * u).astype(o_ref.dtype)


def _up_proj(hn, w_up, *, bm=1024, bn=1024):
    n = hn.shape[0]
    return pl.pallas_call(
        _up_kernel,
        grid=(n // bm, D_FF // bn),
        in_specs=[pl.BlockSpec((bm, D_MODEL), lambda i, j: (i, 0)),
                  pl.BlockSpec((D_MODEL, bn), lambda i, j: (0, j))],
        out_specs=pl.BlockSpec((bm, bn), lambda i, j: (i, j)),
        out_shape=jax.ShapeDtypeStruct((n, D_FF), BF16),
        compiler_params=_cparams(("arbitrary", "arbitrary")),
        name="up_proj",
    )(hn, w_up)


def _down_kernel(a_ref, b_ref, r_ref, o_ref, acc_ref):
    kk = pl.program_id(2)

    @pl.when(kk == 0)
    def _():
        acc_ref[...] = r_ref[...]

    acc_ref[...] += jnp.dot(a_ref[...], b_ref[...], preferred_element_type=F32)

    @pl.when(kk == pl.num_programs(2) - 1)
    def _():
        o_ref[...] = acc_ref[...]


def _down_proj(u, w_down, resid, *, bm=1024, bn=1024, bk=2048):
    n = u.shape[0]
    return pl.pallas_call(
        _down_kernel,
        grid=(n // bm, D_MODEL // bn, D_FF // bk),
        in_specs=[pl.BlockSpec((bm, bk), lambda i, j, k: (i, k)),
                  pl.BlockSpec((bk, bn), lambda i, j, k: (k, j)),
                  pl.BlockSpec((bm, bn), lambda i, j, k: (i, j))],
        out_specs=pl.BlockSpec((bm, bn), lambda i, j, k: (i, j)),
        out_shape=jax.ShapeDtypeStruct((n, D_MODEL), F32),
        scratch_shapes=[pltpu.VMEM((bm, bn), F32)],
        compiler_params=_cparams(("arbitrary", "arbitrary", "arbitrary")),
        name="down_proj",
    )(u, w_down, resid)


def _pad_lanes(vec):
    return jnp.pad(vec, (0, GATE_LANES - vec.shape[0])).reshape(1, GATE_LANES)


def kernel(x, norm1_g, w_in, mlstm_i_bias, mlstm_f_bias, mlstm_norm_g, gdn_conv_w, gdn_a_log,
           gdn_dt_bias, gdn_norm_g, w_out, norm2_g, w_up, w_down, norm_f_g):
    batch, seq, _ = x.shape
    depth = w_in.shape[0]
    n = batch * seq
    h = x.reshape(n, D_MODEL)
    gate0 = 2 * M_HEADS * M_QK_DIM + 2 * M_WIDTH
    gqkv0 = gate0 + 2 * M_HEADS
    ga0 = gqkv0 + 4 * G_WIDTH
    hg = 16
    for l in range(depth):
        wl = w_in[l]
        w_big = jnp.concatenate([wl[:, :gate0], wl[:, gqkv0:ga0]], axis=1).astype(BF16)
        w_gates = jnp.pad(jnp.concatenate([wl[:, gate0:gqkv0], wl[:, ga0:]], axis=1),
                          ((0, 0), (0, GATE_LANES - N_GATES))).astype(BF16)
        bias_vec = _pad_lanes(jnp.concatenate([mlstm_i_bias[l], mlstm_f_bias[l], gdn_dt_bias[l]]))
        alog_vec = _pad_lanes(jnp.concatenate([jnp.zeros((2 * M_HEADS,), F32), gdn_a_log[l]]))
        cw = gdn_conv_w[l].reshape(CONV_K, 3, G_HEADS // hg, hg * G_HEAD_DIM)
        cw = jnp.pad(cw.transpose(2, 0, 1, 3).reshape(G_HEADS // hg, CONV_K, 3 * hg * G_HEAD_DIM),
                     ((0, 0), (0, 8 - CONV_K), (0, 0)))

        xn, gates, gates_t = _norm_gates(h, norm1_g[l].reshape(1, D_MODEL), w_gates, bias_vec, alog_vec)
        proj = _matmul(xn, w_big, bm=1024, bn=1024, out_dtype=F32, name="in_proj")
        mix_m = _mlstm(proj, gates, gates_t, mlstm_norm_g[l], batch=batch, seq=seq)
        mix_g = _gdn(proj, gates, gates_t, cw, gdn_norm_g[l].reshape(1, G_HEAD_DIM),
                     batch=batch, seq=seq, hg=hg)
        h = _out_proj(mix_m, mix_g, w_out[l].astype(BF16), h)
        hn = _rmsnorm(h, norm2_g[l].reshape(1, D_MODEL), BF16, name="norm2")
        up = _up_proj(hn, w_up[l].astype(BF16))
        h = _down_proj(up, w_down[l].astype(BF16), h)
    out = _rmsnorm(h, norm_f_g.reshape(1, D_MODEL), F32, name="norm_f")
    return out.reshape(batch, seq, D_MODEL)
```
